```python
import math
import jax, jax.numpy as jnp
from jax import lax
import numpy as np

D_MODEL = 2048
BATCH = 4
SEQ = 4096
DEPTH = 1
DEC_BATCH = 1
DEC_SEQ = 8192
PAST_LEN = 128

HEAD_DIM = 64
N_HEADS_A = 16
N_HEADS_B = 16
N_KV_B = 4
WIDTH_A = N_HEADS_A * HEAD_DIM
WIDTH_B = N_HEADS_B * HEAD_DIM
MIX_WIDTH = WIDTH_A + WIDTH_B
KV_WIDTH_B = N_KV_B * HEAD_DIM
IN_WIDTH = 3 * WIDTH_A + WIDTH_B + 2 * KV_WIDTH_B
DILATED_PATTERNS = ((128, 1), (512, 4), (2048, 16))
LOCAL_WINDOW = 128
PEER_HEADS = 8
PEER_NKEYS = 128
PEER_DKEY = 256
PEER_TOPK = 16
N_EXPERTS = PEER_NKEYS * PEER_NKEYS
TOKEN_CHUNK = 128
RMS_EPS = 1e-6
NEG = -1e30

kernel_name = "hymba_dilated_swa_peer_encoder"


def rmsnorm(x, g):
    xf = x.astype(jnp.float32)
    y = xf * lax.rsqrt(jnp.mean(xf * xf, axis=-1, keepdims=True) + RMS_EPS)
    return (y * g.astype(jnp.float32)).astype(x.dtype)


def alibi_slopes():
    n = N_HEADS_A + N_HEADS_B
    s = 2.0 ** (-8.0 * jnp.arange(1, n + 1, dtype=jnp.float32) / n)
    return s[0::2], s[1::2]


def banded_attn(q, k, v, slopes, half, step, sink=None):
    B, L, Hq, dh = q.shape
    Hk = k.shape[2]
    G = Hq // Hk
    W = half
    nb = -(-L // W)
    Lp = nb * W
    pad = Lp - L
    q = jnp.pad(q, ((0, 0), (0, pad), (0, 0), (0, 0)))
    kp = jnp.pad(k, ((0, 0), (W, pad + W), (0, 0), (0, 0))).reshape(B, nb + 2, W, Hk, dh)
    vp = jnp.pad(v, ((0, 0), (W, pad + W), (0, 0), (0, 0))).reshape(B, nb + 2, W, Hk, dh)
    kwin = jnp.concatenate([kp[:, :-2], kp[:, 1:-1], kp[:, 2:]], axis=2)
    vwin = jnp.concatenate([vp[:, :-2], vp[:, 1:-1], vp[:, 2:]], axis=2)
    qb = q.reshape(B, nb, W, Hk, G, dh)
    s = jnp.einsum('bnqkgd,bnskd->bnkgqs', qb, kwin).astype(jnp.float32) / math.sqrt(dh)
    rel = jnp.arange(3 * W)[None, :] - W - jnp.arange(W)[:, None]
    kpos = jnp.arange(nb)[:, None] * W + jnp.arange(3 * W)[None, :] - W
    valid = (jnp.abs(rel) <= W)[None] & ((kpos >= 0) & (kpos < L))[:, None, :]
    dist = (jnp.abs(rel) * step).astype(jnp.float32)
    bias = -slopes.astype(jnp.float32).reshape(Hk, G)[:, :, None, None] * dist
    s = jnp.where(valid[None, :, None, None], s + bias[None, None], NEG)
    m = jnp.max(s, axis=-1)
    if sink is not None:
        sk = sink.astype(jnp.float32).reshape(Hk, G)[:, :, None]
        m = jnp.maximum(m, sk)
    e = jnp.exp(s - m[..., None])
    den = jnp.sum(e, axis=-1)
    if sink is not None:
        den = den + jnp.exp(sk - m)
    lse = m + jnp.log(den)
    p = e / den[..., None]
    o = jnp.einsum('bnkgqs,bnskd->bnqkgd', p.astype(vwin.dtype), vwin)
    o = o.reshape(B, Lp, Hq, dh)[:, :L]
    lse = lse.transpose(0, 1, 4, 2, 3).reshape(B, Lp, Hq)[:, :L]
    return o, lse


def dilated_attention(q, k, v, slopes):
    B, S, H, dh = q.shape
    outs, lses = [], []
    for window, d in DILATED_PATTERNS:
        half = (window // 2) // d
        def split(t):
            return t.reshape(B, S // d, d, H, dh).transpose(0, 2, 1, 3, 4).reshape(B * d, S // d, H, dh)
        o, lse = banded_attn(split(q), split(k), split(v), slopes, half, d)
        outs.append(o.reshape(B, d, S // d, H, dh).transpose(0, 2, 1, 3, 4).reshape(B, S, H, dh).astype(jnp.float32))
        lses.append(lse.reshape(B, d, S // d, H).transpose(0, 2, 1, 3).reshape(B, S, H))
    w = jax.nn.softmax(jnp.stack(lses), axis=0)
    out = jnp.einsum('pbsh,pbshd->bshd', w, jnp.stack(outs))
    return out.astype(q.dtype)


def peer(h, w_pq, sub_keys, expert_u, expert_v):
    B, S, D = h.shape
    T = B * S
    hc = h.reshape(T // TOKEN_CHUNK, TOKEN_CHUNK, D)
    K = PEER_TOPK

    def chunk(xc):
        C = xc.shape[0]
        q = (xc @ w_pq).reshape(C, PEER_HEADS, 2, PEER_DKEY // 2)
        sc = jnp.einsum('chpd,hpnd->chpn', q, sub_keys).astype(jnp.float32)
        v1, i1 = lax.top_k(sc[:, :, 0], K)
        v2, i2 = lax.top_k(sc[:, :, 1], K)
        cand = (v1[..., :, None] + v2[..., None, :]).reshape(C, PEER_HEADS, K * K)
        cid = (i1[..., :, None] * PEER_NKEYS + i2[..., None, :]).reshape(C, PEER_HEADS, K * K)
        top, pos = lax.top_k(cand, K)
        eid = jnp.take_along_axis(cid, pos, axis=-1)
        g = jax.nn.softmax(top, axis=-1)
        ue = expert_u[eid]
        act = jax.nn.gelu(jnp.einsum('cd,chkd->chk', xc, ue).astype(jnp.float32), approximate=False)
        ve = expert_v[eid]
        return jnp.einsum('chk,chkd->cd', (g * act).astype(xc.dtype), ve)

    return lax.map(chunk, hc).reshape(B, S, D)


def trunk(x, norm1_g, w_in, a_out_g, b_out_g, sink, w_out, norm2_g, w_pq, sub_keys, expert_u, expert_v, normf_g):
    B, S, _ = x.shape
    slopes_a, slopes_b = alibi_slopes()
    cuts = [WIDTH_A, 2 * WIDTH_A, 3 * WIDTH_A, 3 * WIDTH_A + WIDTH_B, 3 * WIDTH_A + WIDTH_B + KV_WIDTH_B]
    for l in range(DEPTH):
        xn = rmsnorm(x, norm1_g[l])
        proj = xn @ w_in[l]
        qa, ka, va, qb, kb, vb = jnp.split(proj, cuts, axis=-1)
        qa = qa.reshape(B, S, N_HEADS_A, HEAD_DIM)
        ka = ka.reshape(B, S, N_HEADS_A, HEAD_DIM)
        va = va.reshape(B, S, N_HEADS_A, HEAD_DIM)
        qb = qb.reshape(B, S, N_HEADS_B, HEAD_DIM)
        kb = kb.reshape(B, S, N_KV_B, HEAD_DIM)
        vb = vb.reshape(B, S, N_KV_B, HEAD_DIM)
        oa = dilated_attention(qa, ka, va, slopes_a).reshape(B, S, WIDTH_A)
        ob, _ = banded_attn(qb, kb, vb, slopes_b, LOCAL_WINDOW, 1, sink[l])
        ob = ob.reshape(B, S, WIDTH_B)
        mixed = jnp.concatenate([rmsnorm(oa, a_out_g[l]), rmsnorm(ob, b_out_g[l])], axis=-1)
        x = x + mixed @ w_out[l]
        x = x + peer(rmsnorm(x, norm2_g[l]), w_pq[l], sub_keys[l], expert_u[l], expert_v[l])
    return rmsnorm(x, normf_g)


def setup_inputs(seed: int = 0) -> dict:
    key = jax.random.key(seed)
    ks = jax.random.split(key, 16)
    f32 = jnp.float32
    D = D_MODEL
    nrm = lambda k, shape, sc: jax.random.normal(k, shape, f32) * sc
    return {
        "x_prompt": nrm(ks[0], (BATCH, SEQ, D), 1.0),
        "x_sample": nrm(ks[1], (DEC_BATCH, DEC_SEQ, D), 1.0),
        "norm1_g": 1.0 + nrm(ks[2], (DEPTH, D), 0.02),
        "w_in": nrm(ks[3], (DEPTH, D, IN_WIDTH), D ** -0.5),
        "a_out_g": 1.0 + nrm(ks[4], (DEPTH, WIDTH_A), 0.02),
        "b_out_g": 1.0 + nrm(ks[5], (DEPTH, WIDTH_B), 0.02),
        "sink": nrm(ks[6], (DEPTH, N_HEADS_B), 0.1),
        "w_out": nrm(ks[7], (DEPTH, MIX_WIDTH, D), MIX_WIDTH ** -0.5),
        "norm2_g": 1.0 + nrm(ks[8], (DEPTH, D), 0.02),
        "w_pq": nrm(ks[9], (DEPTH, D, PEER_HEADS * PEER_DKEY), D ** -0.5),
        "sub_keys": nrm(ks[10], (DEPTH, PEER_HEADS, 2, PEER_NKEYS, PEER_DKEY // 2), (PEER_DKEY // 2) ** -0.5),
        "expert_u": nrm(ks[11], (DEPTH, N_EXPERTS, D), D ** -0.5),
        "expert_v": nrm(ks[12], (DEPTH, N_EXPERTS, D), 0.1),
        "normf_g": 1.0 + nrm(ks[13], (D,), 0.02),
    }


def reference(x_prompt, x_sample, norm1_g, w_in, a_out_g, b_out_g, sink, w_out, norm2_g, w_pq, sub_keys, expert_u, expert_v, normf_g):
    y_prompt = trunk(x_prompt, norm1_g, w_in, a_out_g, b_out_g, sink, w_out, norm2_g, w_pq, sub_keys, expert_u, expert_v, normf_g)
    y_sample = trunk(x_sample, norm1_g, w_in, a_out_g, b_out_g, sink, w_out, norm2_g, w_pq, sub_keys, expert_u, expert_v, normf_g)
    return (y_prompt, y_sample)
```

```python
import functools
import math

import jax
import jax.numpy as jnp
from jax import lax
from jax.experimental import pallas as pl
from jax.experimental.pallas import tpu as pltpu

F32 = jnp.float32
BF16 = jnp.bfloat16

HEAD_DIM = 64
N_HEADS_A = 16
N_HEADS_B = 16
N_KV_B = 4
WIDTH_A = N_HEADS_A * HEAD_DIM
WIDTH_B = N_HEADS_B * HEAD_DIM
KV_WIDTH_B = N_KV_B * HEAD_DIM
DILATED_PATTERNS = ((128, 1), (512, 4), (2048, 16))
LOCAL_WINDOW = 128
PEER_HEADS = 8
PEER_NKEYS = 128
PEER_TOPK = 16
RMS_EPS = 1e-6
MASKED_DIST = -1e30

VMEM_LIMIT_BYTES = 56 * 1024 * 1024
QUERY_ROWS = 128


def _alibi_slopes():
    n = N_HEADS_A + N_HEADS_B
    s = [2.0 ** (-8.0 * i / n) for i in range(1, n + 1)]
    return tuple(s[0::2]), tuple(s[1::2])


def _rms(x, g):
    return x * lax.rsqrt(jnp.mean(x * x, axis=-1, keepdims=True) + RMS_EPS) * g


def _gelu(x):
    return 0.5 * x * (1.0 + lax.erf(x * math.sqrt(0.5)))


def _params(*sem):
    return pltpu.CompilerParams(dimension_semantics=sem, vmem_limit_bytes=VMEM_LIMIT_BYTES)


def _norm_proj_kernel(x_ref, g_ref, w_ref, o_ref, xn_ref):
    @pl.when(pl.program_id(1) == 0)
    def _():
        xn_ref[...] = _rms(x_ref[...], g_ref[...]).astype(BF16)

    o_ref[...] = jnp.dot(xn_ref[...], w_ref[...], preferred_element_type=F32).astype(o_ref.dtype)


def _norm_proj(x2, g, w, *, tm, tn):
    T, D = x2.shape
    N = w.shape[1]
    return pl.pallas_call(
        _norm_proj_kernel,
        grid=(T // tm, N // tn),
        in_specs=[
            pl.BlockSpec((tm, D), lambda i, j: (i, 0)),
            pl.BlockSpec((1, D), lambda i, j: (0, 0)),
            pl.BlockSpec((D, tn), lambda i, j: (0, j)),
        ],
        out_specs=pl.BlockSpec((tm, tn), lambda i, j: (i, j)),
        out_shape=jax.ShapeDtypeStruct((T, N), BF16),
        scratch_shapes=[pltpu.VMEM((tm, D), BF16)],
        compiler_params=_params("parallel", "arbitrary"),
        name="norm_proj",
    )(x2, g, w)


def _band_attn_kernel(*refs, half, tq, seq_len, n_q_heads, group, slopes, step, with_sink, with_lse):
    q_ref, kl_ref, km_ref, kr_ref, vl_ref, vm_ref, vr_ref = refs[:7]
    pos = 7
    sink_ref = None
    if with_sink:
        sink_ref = refs[pos]
        pos += 1
    o_ref = refs[pos]
    pos += 1
    lse_ref = None
    if with_lse:
        lse_ref = refs[pos]
        pos += 1
    kwin, vwin = refs[pos], refs[pos + 1]

    kwin[0:half] = kl_ref[0]
    kwin[half:half + tq] = km_ref[0]
    kwin[half + tq:] = kr_ref[0]
    vwin[0:half] = vl_ref[0]
    vwin[half:half + tq] = vm_ref[0]
    vwin[half + tq:] = vr_ref[0]

    q0 = pl.program_id(1) * tq
    sb_rows = min(QUERY_ROWS, tq)
    kw = sb_rows + 2 * half
    for sb in range(tq // sb_rows):
        r0 = sb * sb_rows
        rows = lax.broadcasted_iota(jnp.int32, (sb_rows, kw), 0)
        cols = lax.broadcasted_iota(jnp.int32, (sb_rows, kw), 1)
        rel = cols - half - rows
        kpos = q0 + r0 - half + cols
        valid = (jnp.abs(rel) <= half) & (kpos >= 0) & (kpos < seq_len)
        neg_dist = jnp.where(valid, -(jnp.abs(rel) * step).astype(F32), MASKED_DIST)
        for h in range(n_q_heads):
            kv = h // group
            q = q_ref[0, r0:r0 + sb_rows, h * HEAD_DIM:(h + 1) * HEAD_DIM] * 0.125
            k = kwin[r0:r0 + kw, kv * HEAD_DIM:(kv + 1) * HEAD_DIM]
            v = vwin[r0:r0 + kw, kv * HEAD_DIM:(kv + 1) * HEAD_DIM]
            s = lax.dot_general(q, k, (((1,), (1,)), ((), ())), preferred_element_type=F32)
            s = s + slopes[h] * neg_dist
            m = jnp.max(s, axis=1, keepdims=True)
            if with_sink:
                sk = sink_ref[h]
                m = jnp.maximum(m, sk)
            e = jnp.exp(s - m)
            den = jnp.sum(e, axis=1, keepdims=True)
            if with_sink:
                den = den + jnp.exp(sk - m)
            o = jnp.dot(e.astype(BF16), v, preferred_element_type=F32) / den
            o_ref[0, r0:r0 + sb_rows, h * HEAD_DIM:(h + 1) * HEAD_DIM] = o.astype(o_ref.dtype)
            if with_lse:
                lse = m + jnp.log(den)
                lse_ref[0, r0:r0 + sb_rows, h * HEAD_DIM:(h + 1) * HEAD_DIM] = jnp.broadcast_to(
                    lse, (sb_rows, HEAD_DIM))


def _band_attn(qkv, *, q_col, k_col, v_col, n_q_heads, n_kv_heads, half, step, slopes, sink=None, with_lse):
    NS, L, _ = qkv.shape
    wq = n_q_heads * HEAD_DIM
    wk = n_kv_heads * HEAD_DIM
    tq = min(256, L)
    r = tq // half
    n_halo = L // half
    with_sink = sink is not None

    def main_map(col):
        return lambda s, i: (s, i, col)

    def left_map(col):
        return lambda s, i: (s, jnp.maximum(i * r - 1, 0), col)

    def right_map(col):
        return lambda s, i: (s, jnp.minimum((i + 1) * r, n_halo - 1), col)

    in_specs = [
        pl.BlockSpec((1, tq, wq), main_map(q_col)),
        pl.BlockSpec((1, half, wk), left_map(k_col)),
        pl.BlockSpec((1, tq, wk), main_map(k_col)),
        pl.BlockSpec((1, half, wk), right_map(k_col)),
        pl.BlockSpec((1, half, wk), left_map(v_col)),
        pl.BlockSpec((1, tq, wk), main_map(v_col)),
        pl.BlockSpec((1, half, wk), right_map(v_col)),
    ]
    args = [qkv] * 7
    if with_sink:
        in_specs.append(pl.BlockSpec(memory_space=pltpu.SMEM))
        args.append(sink)
    out_block = pl.BlockSpec((1, tq, wq), lambda s, i: (s, i, 0))
    out_specs = [out_block]
    out_shape = [jax.ShapeDtypeStruct((NS, L, wq), F32)]
    if with_lse:
        out_specs.append(out_block)
        out_shape.append(jax.ShapeDtypeStruct((NS, L, wq), F32))
    kernel = functools.partial(
        _band_attn_kernel, half=half, tq=tq, seq_len=L, n_q_heads=n_q_heads, group=n_q_heads // n_kv_heads,
        slopes=slopes, step=step, with_sink=with_sink, with_lse=with_lse)
    return pl.pallas_call(
        kernel,
        grid=(NS, L // tq),
        in_specs=in_specs,
        out_specs=out_specs,
        out_shape=out_shape,
        scratch_shapes=[pltpu.VMEM((tq + 2 * half, wk), BF16), pltpu.VMEM((tq + 2 * half, wk), BF16)],
        compiler_params=_params("parallel", "arbitrary"),
        name=f"band_attn_h{half}_s{step}",
    )(*args)


def _mix_out_kernel(o1, l1, o2, l2, o3, l3, ob, x_ref, ga, gb, g2, wa, wb, h_ref, hn_ref):
    a1, a2, a3 = l1[...], l2[...], l3[...]
    mx = jnp.maximum(jnp.maximum(a1, a2), a3)
    e1, e2, e3 = jnp.exp(a1 - mx), jnp.exp(a2 - mx), jnp.exp(a3 - mx)
    oa = (e1 * o1[...] + e2 * o2[...] + e3 * o3[...]) / (e1 + e2 + e3)
    na = _rms(oa, ga[...]).astype(BF16)
    nb = _rms(ob[...], gb[...]).astype(BF16)
    h = x_ref[...] + jnp.dot(na, wa[...], preferred_element_type=F32) + jnp.dot(nb, wb[...], preferred_element_type=F32)
    h_ref[...] = h
    hn_ref[...] = _rms(h, g2[...]).astype(BF16)


def _mix_out(o1, l1, o2, l2, o3, l3, ob, x2, ga, gb, g2, w_out, *, tm):
    T, D = x2.shape
    wa_rows = o1.shape[1]
    wb_rows = ob.shape[1]
    tok = lambda w: pl.BlockSpec((tm, w), lambda i: (i, 0))
    row = lambda w: pl.BlockSpec((1, w), lambda i: (0, 0))
    return pl.pallas_call(
        _mix_out_kernel,
        grid=(T // tm,),
        in_specs=[tok(wa_rows)] * 6 + [tok(wb_rows), tok(D), row(wa_rows), row(wb_rows), row(D),
                                      pl.BlockSpec((wa_rows, D), lambda i: (0, 0)),
                                      pl.BlockSpec((wb_rows, D), lambda i: (wa_rows // wb_rows, 0))],
        out_specs=[tok(D), tok(D)],
        out_shape=[jax.ShapeDtypeStruct((T, D), F32), jax.ShapeDtypeStruct((T, D), BF16)],
        compiler_params=_params("parallel"),
        name="mix_out",
    )(o1, l1, o2, l2, o3, l3, ob, x2, ga, gb, g2, w_out, w_out)


def _top_sorted(s, k):
    vals = []
    cur = s
    for i in range(k):
        mx = jnp.max(cur, axis=0, keepdims=True)
        vals.append(mx)
        if i + 1 < k:
            cur = jnp.where(cur >= mx, -jnp.inf, cur)
    return vals


def _peer_scores_kernel(hn_ref, wq_ref, sk_ref, na_ref, ea_ref, s2_ref, es2_ref, qt_ref):
    qt_ref[...] = lax.dot_general(wq_ref[...], hn_ref[...], (((1,), (1,)), ((), ())),
                                  preferred_element_type=F32).astype(BF16)
    nk = PEER_NKEYS

    def head(h, carry):
        base = pl.multiple_of(h * 2 * nk, 2 * nk)
        s1 = jnp.dot(sk_ref[h, 0], qt_ref[pl.ds(base, nk), :], preferred_element_type=F32)
        s2 = jnp.dot(sk_ref[h, 1], qt_ref[pl.ds(base + nk, nk), :], preferred_element_type=F32)
        v1 = _top_sorted(s1, PEER_TOPK)
        v2 = _top_sorted(s2, PEER_TOPK)
        ranks = lax.broadcasted_iota(jnp.int32, (PEER_TOPK, s2.shape[1]), 0)
        v2_all = jnp.broadcast_to(v2[PEER_TOPK - 1], ranks.shape)
        for b in range(PEER_TOPK - 1):
            v2_all = jnp.where(ranks == b, v2[b], v2_all)
        v2_top = v2_all[0:8]
        cands = [v1[0] + v2_all] + [v1[a] + v2_top for a in range(1, PEER_TOPK)]
        cand = jnp.concatenate(cands, axis=0)
        top = _top_sorted(cand, PEER_TOPK)
        m = top[0]
        z = jnp.exp(top[0] - m)
        for i in range(1, PEER_TOPK):
            z = z + jnp.exp(top[i] - m)
        tau = top[PEER_TOPK - 1]
        na_ref[h] = -s1
        ea_ref[h] = jnp.exp(s1 - v1[0])
        s2_ref[h] = s2 - tau
        es2_ref[h] = jnp.exp(s2 - v2[0]) / z
        return carry

    lax.fori_loop(0, PEER_HEADS, head, 0)


def _peer_scores(hn, wq_t, sk, *, tm):
    T, D = hn.shape
    Q = wq_t.shape[0]
    out_block = pl.BlockSpec((PEER_HEADS, PEER_NKEYS, tm), lambda i: (0, 0, i))
    out_sds = jax.ShapeDtypeStruct((PEER_HEADS, PEER_NKEYS, T), F32)
    return pl.pallas_call(
        _peer_scores_kernel,
        grid=(T // tm,),
        in_specs=[
            pl.BlockSpec((tm, D), lambda i: (i, 0)),
            pl.BlockSpec((Q, D), lambda i: (0, 0)),
            pl.BlockSpec(sk.shape, lambda i: (0, 0, 0, 0)),
        ],
        out_specs=[out_block] * 4,
        out_shape=[out_sds] * 4,
        scratch_shapes=[pltpu.VMEM((Q, tm), BF16)],
        compiler_params=_params("parallel"),
        name="peer_scores",
    )(hn, wq_t, sk)


def _peer_dense_kernel(hn_ref, u_ref, vt_ref, na_ref, ea_ref, s2_ref, es2_ref, h_ref, gf_ref, y_ref, acc_ref, *, n_i1):
    e = pl.program_id(1)

    @pl.when(e == 0)
    def _():
        acc_ref[...] = jnp.zeros_like(acc_ref)

    act_t = lax.dot_general(u_ref[...], hn_ref[...], (((1,), (1,)), ((), ())), preferred_element_type=F32)
    nk = PEER_NKEYS
    parts = []
    for ii in range(n_i1):
        i1 = e * n_i1 + ii
        wsum = None
        for h in range(PEER_HEADS):
            na = na_ref[h, pl.ds(i1, 1), :]
            ea = ea_ref[h, pl.ds(i1, 1), :]
            w = jnp.where(s2_ref[h] >= na, ea * es2_ref[h], 0.0)
            wsum = w if wsum is None else wsum + w
        a = act_t[ii * nk:(ii + 1) * nk]
        parts.append((wsum * _gelu(a)).astype(BF16))
    p_t = jnp.concatenate(parts, axis=0) if n_i1 > 1 else parts[0]
    acc_ref[...] += jnp.dot(vt_ref[...], p_t, preferred_element_type=F32)

    @pl.when(e == pl.num_programs(1) - 1)
    def _():
        y_ref[...] = _rms(h_ref[...] + acc_ref[...].T, gf_ref[...])


def _peer_dense(hn, u, vt, na, ea, s2, es2, h, gf, *, tm, te):
    T, D = hn.shape
    E = u.shape[0]
    n_i1 = te // PEER_NKEYS
    thr = pl.BlockSpec((PEER_HEADS, PEER_NKEYS, tm), lambda i, e: (0, 0, i))
    return pl.pallas_call(
        functools.partial(_peer_dense_kernel, n_i1=n_i1),
        grid=(T // tm, E // te),
        in_specs=[
            pl.BlockSpec((tm, D), lambda i, e: (i, 0)),
            pl.BlockSpec((te, D), lambda i, e: (e, 0)),
            pl.BlockSpec((D, te), lambda i, e: (0, e)),
            thr, thr, thr, thr,
            pl.BlockSpec((tm, D), lambda i, e: (i, 0)),
            pl.BlockSpec((1, D), lambda i, e: (0, 0)),
        ],
        out_specs=pl.BlockSpec((tm, D), lambda i, e: (i, 0)),
        out_shape=jax.ShapeDtypeStruct((T, D), F32),
        scratch_shapes=[pltpu.VMEM((D, tm), F32)],
        compiler_params=_params("parallel", "arbitrary"),
        name="peer_dense",
    )(hn, u, vt, na, ea, s2, es2, h, gf)


def _split_residues(a, d):
    B, S, C = a.shape
    return a.reshape(B, S // d, d, C).transpose(0, 2, 1, 3).reshape(B * d, S // d, C)


def _merge_residues(a, d):
    BD, L, C = a.shape
    B = BD // d
    return a.reshape(B, d, L, C).transpose(0, 2, 1, 3).reshape(B, L * d, C)


def _trunk(x, w, slopes_a, slopes_b):
    B, S, D = x.shape
    T = B * S
    x2 = x.reshape(T, D)
    proj = _norm_proj(x2, w["norm1_g"], w["w_in"], tm=512, tn=1536).reshape(B, S, -1)

    outs = []
    for window, d in DILATED_PATTERNS:
        half = (window // 2) // d
        qkv = proj if d == 1 else _split_residues(proj[:, :, :3 * WIDTH_A], d)
        o, lse = _band_attn(qkv, q_col=0, k_col=1, v_col=2, n_q_heads=N_HEADS_A, n_kv_heads=N_HEADS_A,
                            half=half, step=d, slopes=slopes_a, with_lse=True)
        if d != 1:
            o, lse = _merge_residues(o, d), _merge_residues(lse, d)
        outs += [o.reshape(T, WIDTH_A), lse.reshape(T, WIDTH_A)]
    kb_col = (3 * WIDTH_A + WIDTH_B) // KV_WIDTH_B
    (ob,) = _band_attn(proj, q_col=3, k_col=kb_col, v_col=kb_col + 1, n_q_heads=N_HEADS_B, n_kv_heads=N_KV_B,
                       half=LOCAL_WINDOW, step=1, slopes=slopes_b, sink=w["sink"], with_lse=False)

    h, hn = _mix_out(*outs, ob.reshape(T, WIDTH_B), x2, w["a_out_g"], w["b_out_g"], w["norm2_g"], w["w_out"], tm=256)
    na, ea, s2, es2 = _peer_scores(hn, w["w_pq_t"], w["sub_keys"], tm=256)
    y = _peer_dense(hn, w["expert_u"], w["expert_v_t"], na, ea, s2, es2, h, w["normf_g"], tm=512, te=512)
    return y.reshape(B, S, D)


def _prep_weights(norm1_g, w_in, a_out_g, b_out_g, sink, w_out, norm2_g, w_pq, sub_keys, expert_u, expert_v, normf_g):
    assert norm1_g.shape[0] == 1, "single-layer trunk"
    return {
        "norm1_g": norm1_g[0][None].astype(F32),
        "w_in": w_in[0].astype(BF16),
        "a_out_g": a_out_g[0][None].astype(F32),
        "b_out_g": b_out_g[0][None].astype(F32),
        "sink": sink[0].astype(F32),
        "w_out": w_out[0].astype(BF16),
        "norm2_g": norm2_g[0][None].astype(F32),
        "w_pq_t": w_pq[0].T.astype(BF16),
        "sub_keys": sub_keys[0].astype(BF16),
        "expert_u": expert_u[0].astype(BF16),
        "expert_v_t": expert_v[0].T.astype(BF16),
        "normf_g": normf_g[None].astype(F32),
    }


def kernel(x_prompt, x_sample, norm1_g, w_in, a_out_g, b_out_g, sink, w_out, norm2_g, w_pq, sub_keys, expert_u,
           expert_v, normf_g):
    slopes_a, slopes_b = _alibi_slopes()
    w = _prep_weights(norm1_g, w_in, a_out_g, b_out_g, sink, w_out, norm2_g, w_pq, sub_keys, expert_u, expert_v,
                      normf_g)
    return (_trunk(x_prompt, w, slopes_a, slopes_b), _trunk(x_sample, w, slopes_a, slopes_b))
```

```python
import functools
import math

import jax
import jax.numpy as jnp
from jax import lax
from jax.experimental import pallas as pl
from jax.experimental.pallas import tpu as pltpu

F32 = jnp.float32
BF16 = jnp.bfloat16

HEAD_DIM = 64
N_HEADS_A = 16
N_HEADS_B = 16
N_KV_B = 4
WIDTH_A = N_HEADS_A * HEAD_DIM
WIDTH_B = N_HEADS_B * HEAD_DIM
KV_WIDTH_B = N_KV_B * HEAD_DIM
DILATED_PATTERNS = ((128, 1), (512, 4), (2048, 16))
LOCAL_WINDOW = 128
PEER_HEADS = 8
PEER_NKEYS = 128
PEER_TOPK = 16
RMS_EPS = 1e-6
MASKED_DIST = -1e30

VMEM_LIMIT_BYTES = 56 * 1024 * 1024
QUERY_ROWS = 128
HEADS_PER_GROUP = 8


def _alibi_slopes():
    n = N_HEADS_A + N_HEADS_B
    s = [2.0 ** (-8.0 * i / n) for i in range(1, n + 1)]
    return tuple(s[0::2]), tuple(s[1::2])


def _rms(x, g):
    return x * lax.rsqrt(jnp.mean(x * x, axis=-1, keepdims=True) + RMS_EPS) * g


def _gelu(x):
    return 0.5 * x * (1.0 + lax.erf(x * math.sqrt(0.5)))


def _params(*sem, flags=None):
    return pltpu.CompilerParams(dimension_semantics=sem, vmem_limit_bytes=VMEM_LIMIT_BYTES, flags=flags)


def _norm_proj_kernel(x_ref, g_ref, w_ref, o_ref, xn_ref):
    @pl.when(pl.program_id(1) == 0)
    def _():
        xn_ref[...] = _rms(x_ref[...], g_ref[...]).astype(BF16)

    o_ref[...] = jnp.dot(xn_ref[...], w_ref[...], preferred_element_type=F32).astype(o_ref.dtype)


def _norm_proj(x2, g, w, *, tm, tn):
    T, D = x2.shape
    N = w.shape[1]
    return pl.pallas_call(
        _norm_proj_kernel,
        grid=(T // tm, N // tn),
        in_specs=[
            pl.BlockSpec((tm, D), lambda i, j: (i, 0)),
            pl.BlockSpec((1, D), lambda i, j: (0, 0)),
            pl.BlockSpec((D, tn), lambda i, j: (0, j)),
        ],
        out_specs=pl.BlockSpec((tm, tn), lambda i, j: (i, j)),
        out_shape=jax.ShapeDtypeStruct((T, N), BF16),
        scratch_shapes=[pltpu.VMEM((tm, D), BF16)],
        compiler_params=_params("parallel", "arbitrary"),
        name="norm_proj",
    )(x2, g, w)


def _band_attn_kernel(*refs, half, tq, seq_len, n_q_heads, group, slopes, step, with_sink, with_lse):
    q_ref, kl_ref, km_ref, kr_ref, vl_ref, vm_ref, vr_ref = refs[:7]
    pos = 7
    sink_ref = None
    if with_sink:
        sink_ref = refs[pos]
        pos += 1
    o_ref = refs[pos]
    pos += 1
    lse_ref = None
    if with_lse:
        lse_ref = refs[pos]
        pos += 1
    kwin, vwin = refs[pos], refs[pos + 1]

    kwin[0:half] = kl_ref[0]
    kwin[half:half + tq] = km_ref[0]
    kwin[half + tq:] = kr_ref[0]
    vwin[0:half] = vl_ref[0]
    vwin[half:half + tq] = vm_ref[0]
    vwin[half + tq:] = vr_ref[0]

    q0 = pl.program_id(1) * tq
    sb_rows = min(QUERY_ROWS, tq)
    kw = sb_rows + 2 * half
    for sb in range(tq // sb_rows):
        r0 = sb * sb_rows
        rows = lax.broadcasted_iota(jnp.int32, (sb_rows, kw), 0)
        cols = lax.broadcasted_iota(jnp.int32, (sb_rows, kw), 1)
        rel = cols - half - rows
        kpos = q0 + r0 - half + cols
        valid = (jnp.abs(rel) <= half) & (kpos >= 0) & (kpos < seq_len)
        neg_dist = jnp.where(valid, -(jnp.abs(rel) * step).astype(F32), MASKED_DIST)
        for h0 in range(0, n_q_heads, HEADS_PER_GROUP):
            heads = range(h0, min(h0 + HEADS_PER_GROUP, n_q_heads))
            scores = []
            for h in heads:
                kv = h // group
                q = q_ref[0, r0:r0 + sb_rows, h * HEAD_DIM:(h + 1) * HEAD_DIM] * 0.125
                k = kwin[r0:r0 + kw, kv * HEAD_DIM:(kv + 1) * HEAD_DIM]
                s = lax.dot_general(q, k, (((1,), (1,)), ((), ())), preferred_element_type=F32)
                scores.append(s + slopes[h] * neg_dist)
            s = jnp.stack(scores)
            m = jnp.max(s, axis=2, keepdims=True)
            if with_sink:
                sk = jnp.stack([jnp.full((1, 1), sink_ref[h], F32) for h in heads])
                m = jnp.maximum(m, sk)
            e = jnp.exp(s - m)
            den = jnp.sum(e, axis=2, keepdims=True)
            if with_sink:
                den = den + jnp.exp(sk - m)
            p = e.astype(BF16)
            inv = 1.0 / den
            if with_lse:
                lse = m + jnp.log(den)
            for i, h in enumerate(heads):
                kv = h // group
                v = vwin[r0:r0 + kw, kv * HEAD_DIM:(kv + 1) * HEAD_DIM]
                o = jnp.dot(p[i], v, preferred_element_type=F32) * inv[i]
                o_ref[0, r0:r0 + sb_rows, h * HEAD_DIM:(h + 1) * HEAD_DIM] = o.astype(o_ref.dtype)
                if with_lse:
                    lse_ref[0, r0:r0 + sb_rows, h * HEAD_DIM:(h + 1) * HEAD_DIM] = jnp.broadcast_to(
                        lse[i], (sb_rows, HEAD_DIM))


def _band_attn(qkv, *, q_col, k_col, v_col, n_q_heads, n_kv_heads, half, step, slopes, sink=None, with_lse):
    NS, L, _ = qkv.shape
    wq = n_q_heads * HEAD_DIM
    wk = n_kv_heads * HEAD_DIM
    tq = min(256, L)
    r = tq // half
    n_halo = L // half
    with_sink = sink is not None

    def main_map(col):
        return lambda s, i: (s, i, col)

    def left_map(col):
        return lambda s, i: (s, jnp.maximum(i * r - 1, 0), col)

    def right_map(col):
        return lambda s, i: (s, jnp.minimum((i + 1) * r, n_halo - 1), col)

    in_specs = [
        pl.BlockSpec((1, tq, wq), main_map(q_col)),
        pl.BlockSpec((1, half, wk), left_map(k_col)),
        pl.BlockSpec((1, tq, wk), main_map(k_col)),
        pl.BlockSpec((1, half, wk), right_map(k_col)),
        pl.BlockSpec((1, half, wk), left_map(v_col)),
        pl.BlockSpec((1, tq, wk), main_map(v_col)),
        pl.BlockSpec((1, half, wk), right_map(v_col)),
    ]
    args = [qkv] * 7
    if with_sink:
        in_specs.append(pl.BlockSpec(memory_space=pltpu.SMEM))
        args.append(sink)
    out_block = pl.BlockSpec((1, tq, wq), lambda s, i: (s, i, 0))
    out_specs = [out_block]
    out_shape = [jax.ShapeDtypeStruct((NS, L, wq), F32)]
    if with_lse:
        out_specs.append(out_block)
        out_shape.append(jax.ShapeDtypeStruct((NS, L, wq), F32))
    kernel = functools.partial(
        _band_attn_kernel, half=half, tq=tq, seq_len=L, n_q_heads=n_q_heads, group=n_q_heads // n_kv_heads,
        slopes=slopes, step=step, with_sink=with_sink, with_lse=with_lse)
    return pl.pallas_call(
        kernel,
        grid=(NS, L // tq),
        in_specs=in_specs,
        out_specs=out_specs,
        out_shape=out_shape,
        scratch_shapes=[pltpu.VMEM((tq + 2 * half, wk), BF16), pltpu.VMEM((tq + 2 * half, wk), BF16)],
        compiler_params=_params("parallel", "arbitrary"),
        name=f"band_attn_h{half}_s{step}",
    )(*args)


def _mix_out_kernel(o1, l1, o2, l2, o3, l3, ob, x_ref, ga, gb, g2, wa, wb, h_ref, hn_ref):
    a1, a2, a3 = l1[...], l2[...], l3[...]
    mx = jnp.maximum(jnp.maximum(a1, a2), a3)
    e1, e2, e3 = jnp.exp(a1 - mx), jnp.exp(a2 - mx), jnp.exp(a3 - mx)
    oa = (e1 * o1[...] + e2 * o2[...] + e3 * o3[...]) / (e1 + e2 + e3)
    na = _rms(oa, ga[...]).astype(BF16)
    nb = _rms(ob[...], gb[...]).astype(BF16)
    h = x_ref[...] + jnp.dot(na, wa[...], preferred_element_type=F32) + jnp.dot(nb, wb[...], preferred_element_type=F32)
    h_ref[...] = h
    hn_ref[...] = _rms(h, g2[...]).astype(BF16)


def _mix_out(o1, l1, o2, l2, o3, l3, ob, x2, ga, gb, g2, w_out, *, tm):
    T, D = x2.shape
    wa_rows = o1.shape[1]
    wb_rows = ob.shape[1]
    tok = lambda w: pl.BlockSpec((tm, w), lambda i: (i, 0))
    row = lambda w: pl.BlockSpec((1, w), lambda i: (0, 0))
    return pl.pallas_call(
        _mix_out_kernel,
        grid=(T // tm,),
        in_specs=[tok(wa_rows)] * 6 + [tok(wb_rows), tok(D), row(wa_rows), row(wb_rows), row(D),
                                      pl.BlockSpec((wa_rows, D), lambda i: (0, 0)),
                                      pl.BlockSpec((wb_rows, D), lambda i: (wa_rows // wb_rows, 0))],
        out_specs=[tok(D), tok(D)],
        out_shape=[jax.ShapeDtypeStruct((T, D), F32), jax.ShapeDtypeStruct((T, D), BF16)],
        compiler_params=_params("parallel"),
        name="mix_out",
    )(o1, l1, o2, l2, o3, l3, ob, x2, ga, gb, g2, w_out, w_out)


def _top_sorted(s, k):
    vals = []
    cur = s
    for i in range(k):
        mx = jnp.max(cur, axis=0, keepdims=True)
        vals.append(mx)
        if i + 1 < k:
            cur = jnp.where(cur >= mx, -jnp.inf, cur)
    return vals


def _peer_scores_kernel(hn_ref, wq_ref, sk_ref, na_ref, ea_ref, s2_ref, es2_ref, qt_ref):
    qt_ref[...] = lax.dot_general(wq_ref[...], hn_ref[...], (((1,), (1,)), ((), ())),
                                  preferred_element_type=F32).astype(BF16)
    nk = PEER_NKEYS

    def head(h, carry):
        base = pl.multiple_of(h * 2 * nk, 2 * nk)
        s1 = jnp.dot(sk_ref[h, 0], qt_ref[pl.ds(base, nk), :], preferred_element_type=F32)
        s2 = jnp.dot(sk_ref[h, 1], qt_ref[pl.ds(base + nk, nk), :], preferred_element_type=F32)
        v1 = _top_sorted(s1, PEER_TOPK)
        v2 = _top_sorted(s2, PEER_TOPK)
        ranks = lax.broadcasted_iota(jnp.int32, (PEER_TOPK, s2.shape[1]), 0)
        v2_all = jnp.broadcast_to(v2[PEER_TOPK - 1], ranks.shape)
        for b in range(PEER_TOPK - 1):
            v2_all = jnp.where(ranks == b, v2[b], v2_all)
        v2_top = v2_all[0:8]
        cands = [v1[0] + v2_all] + [v1[a] + v2_top for a in range(1, PEER_TOPK)]
        cand = jnp.concatenate(cands, axis=0)
        top = _top_sorted(cand, PEER_TOPK)
        m = top[0]
        z = jnp.exp(top[0] - m)
        for i in range(1, PEER_TOPK):
            z = z + jnp.exp(top[i] - m)
        tau = top[PEER_TOPK - 1]
        na_ref[h] = -s1
        ea_ref[h] = jnp.exp(s1 - v1[0])
        s2_ref[h] = s2 - tau
        es2_ref[h] = jnp.exp(s2 - v2[0]) / z
        return carry

    lax.fori_loop(0, PEER_HEADS, head, 0)


def _peer_scores(hn, wq_t, sk, *, tm):
    T, D = hn.shape
    Q = wq_t.shape[0]
    out_block = pl.BlockSpec((PEER_HEADS, PEER_NKEYS, tm), lambda i: (0, 0, i))
    out_sds = jax.ShapeDtypeStruct((PEER_HEADS, PEER_NKEYS, T), F32)
    return pl.pallas_call(
        _peer_scores_kernel,
        grid=(T // tm,),
        in_specs=[
            pl.BlockSpec((tm, D), lambda i: (i, 0)),
            pl.BlockSpec((Q, D), lambda i: (0, 0)),
            pl.BlockSpec(sk.shape, lambda i: (0, 0, 0, 0)),
        ],
        out_specs=[out_block] * 4,
        out_shape=[out_sds] * 4,
        scratch_shapes=[pltpu.VMEM((Q, tm), BF16)],
        compiler_params=_params("parallel"),
        name="peer_scores",
    )(hn, wq_t, sk)


P_ROWS = 16


def _routing_gelu_slab(act_ref, p_ref, row0, i1, na_ref, ea_ref, s2_ref, es2_ref):
    nk = PEER_NKEYS
    rows = [(na_ref[h, pl.ds(i1, 1), :], ea_ref[h, pl.ds(i1, 1), :]) for h in range(PEER_HEADS)]
    for t in range(nk // P_ROWS):
        k2 = slice(t * P_ROWS, (t + 1) * P_ROWS)
        wsum = None
        for h in range(PEER_HEADS):
            na, ea = rows[h]
            w = jnp.where(s2_ref[h, k2, :] >= na, ea * es2_ref[h, k2, :], 0.0)
            wsum = w if wsum is None else wsum + w
        r = slice(row0 + t * P_ROWS, row0 + (t + 1) * P_ROWS)
        p_ref[r, :] = (wsum * _gelu(act_ref[r, :])).astype(BF16)


def _peer_dense_kernel(hn_ref, u_first_ref, u_next_ref, vt_prev_ref, vt_last_ref, na_ref, ea_ref, s2_ref, es2_ref,
                       h_ref, gf_ref, y_ref, acc_ref, act0, act1, p0, p1, hnt_ref):
    e = pl.program_id(1)
    last = pl.num_programs(1) - 1
    act_bufs, p_bufs = (act0, act1), (p0, p1)
    te, tm = act0.shape
    d_model = acc_ref.shape[0]
    nk = PEER_NKEYS
    n_i1 = te // nk
    n_half = 2
    tw = tm // n_half
    n_vt = 2 * n_i1
    vr = d_model // n_vt

    @pl.when(e == 0)
    def _():
        acc_ref[...] = jnp.zeros_like(acc_ref)
        p0[...] = jnp.zeros_like(p0)
        hnt_ref[...] = hn_ref[...].T
        act0[...] = jnp.dot(u_first_ref[...], hnt_ref[...], preferred_element_type=F32)

    def step(q):
        for j in range(n_i1):
            if j % (n_i1 // n_half) == 0:
                c = j // (n_i1 // n_half)
                act_bufs[1 - q][:, c * tw:(c + 1) * tw] = jnp.dot(
                    u_next_ref[...], hnt_ref[:, c * tw:(c + 1) * tw], preferred_element_type=F32)
            for k in range(2 * j, 2 * j + 2):
                acc_ref[k * vr:(k + 1) * vr] += jnp.dot(vt_prev_ref[k * vr:(k + 1) * vr], p_bufs[q][...],
                                                        preferred_element_type=F32)
            _routing_gelu_slab(act_bufs[q], p_bufs[1 - q], j * nk, e * n_i1 + j, na_ref, ea_ref, s2_ref, es2_ref)

    pl.when(e % 2 == 0)(functools.partial(step, 0))
    pl.when(e % 2 == 1)(functools.partial(step, 1))

    @pl.when(e == last)
    def _():
        peer_t = acc_ref[...] + jnp.dot(vt_last_ref[...], p0[...], preferred_element_type=F32)
        y_ref[...] = _rms(h_ref[...] + peer_t.T, gf_ref[...])


def _peer_dense(hn, u, vt, na, ea, s2, es2, h, gf, *, tm, te):
    T, D = hn.shape
    E = u.shape[0]
    n_blocks = E // te
    assert n_blocks % 2 == 0
    thr = pl.BlockSpec((PEER_HEADS, PEER_NKEYS, tm), lambda i, e: (0, 0, i), pipeline_mode=pl.Buffered(1))
    return pl.pallas_call(
        _peer_dense_kernel,
        grid=(T // tm, n_blocks),
        in_specs=[
            pl.BlockSpec((tm, D), lambda i, e: (i, 0)),
            pl.BlockSpec((te, D), lambda i, e: (0, 0), pipeline_mode=pl.Buffered(1)),
            pl.BlockSpec((te, D), lambda i, e: (jnp.minimum(e + 1, n_blocks - 1), 0)),
            pl.BlockSpec((D, te), lambda i, e: (0, jnp.maximum(e - 1, 0))),
            pl.BlockSpec((D, te), lambda i, e: (0, n_blocks - 1), pipeline_mode=pl.Buffered(1)),
            thr, thr, thr, thr,
            pl.BlockSpec((tm, D), lambda i, e: (i, 0)),
            pl.BlockSpec((1, D), lambda i, e: (0, 0)),
        ],
        out_specs=pl.BlockSpec((tm, D), lambda i, e: (i, 0)),
        out_shape=jax.ShapeDtypeStruct((T, D), F32),
        scratch_shapes=[pltpu.VMEM((D, tm), F32), pltpu.VMEM((te, tm), F32), pltpu.VMEM((te, tm), F32),
                        pltpu.VMEM((te, tm), BF16), pltpu.VMEM((te, tm), BF16), pltpu.VMEM((D, tm), BF16)],
        compiler_params=_params("parallel", "arbitrary"),
        name="peer_dense",
    )(hn, u, u, vt, vt, na, ea, s2, es2, h, gf)


def _split_residues(a, d):
    B, S, C = a.shape
    return a.reshape(B, S // d, d, C).transpose(0, 2, 1, 3).reshape(B * d, S // d, C)


def _merge_residues(a, d):
    BD, L, C = a.shape
    B = BD // d
    return a.reshape(B, d, L, C).transpose(0, 2, 1, 3).reshape(B, L * d, C)


def _trunk(x, w, slopes_a, slopes_b):
    B, S, D = x.shape
    T = B * S
    x2 = x.reshape(T, D)
    proj = _norm_proj(x2, w["norm1_g"], w["w_in"], tm=512, tn=1536).reshape(B, S, -1)

    outs = []
    for window, d in DILATED_PATTERNS:
        half = (window // 2) // d
        qkv = proj if d == 1 else _split_residues(proj[:, :, :3 * WIDTH_A], d)
        o, lse = _band_attn(qkv, q_col=0, k_col=1, v_col=2, n_q_heads=N_HEADS_A, n_kv_heads=N_HEADS_A,
                            half=half, step=d, slopes=slopes_a, with_lse=True)
        if d != 1:
            o, lse = _merge_residues(o, d), _merge_residues(lse, d)
        outs += [o.reshape(T, WIDTH_A), lse.reshape(T, WIDTH_A)]
    kb_col = (3 * WIDTH_A + WIDTH_B) // KV_WIDTH_B
    (ob,) = _band_attn(proj, q_col=3, k_col=kb_col, v_col=kb_col + 1, n_q_heads=N_HEADS_B, n_kv_heads=N_KV_B,
                       half=LOCAL_WINDOW, step=1, slopes=slopes_b, sink=w["sink"], with_lse=False)

    h, hn = _mix_out(*outs, ob.reshape(T, WIDTH_B), x2, w["a_out_g"], w["b_out_g"], w["norm2_g"], w["w_out"], tm=256)
    na, ea, s2, es2 = _peer_scores(hn, w["w_pq_t"], w["sub_keys"], tm=256)
    y = _peer_dense(hn, w["expert_u"], w["expert_v_t"], na, ea, s2, es2, h, w["normf_g"], tm=512, te=512)
    return y.reshape(B, S, D)


def _prep_weights(norm1_g, w_in, a_out_g, b_out_g, sink, w_out, norm2_g, w_pq, sub_keys, expert_u, expert_v, normf_g):
    assert norm1_g.shape[0] == 1, "single-layer trunk"
    return {
        "norm1_g": norm1_g[0][None].astype(F32),
        "w_in": w_in[0].astype(BF16),
        "a_out_g": a_out_g[0][None].astype(F32),
        "b_out_g": b_out_g[0][None].astype(F32),
        "sink": sink[0].astype(F32),
        "w_out": w_out[0].astype(BF16),
        "norm2_g": norm2_g[0][None].astype(F32),
        "w_pq_t": w_pq[0].T.astype(BF16),
        "sub_keys": sub_keys[0].astype(BF16),
        "expert_u": expert_u[0].astype(BF16),
        "expert_v_t": expert_v[0].T.astype(BF16),
        "normf_g": normf_g[None].astype(F32),
    }


def kernel(x_prompt, x_sample, norm1_g, w_in, a_out_g, b_out_g, sink, w_out, norm2_g, w_pq, sub_keys, expert_u,
           expert_v, normf_g):
    slopes_a, slopes_b = _alibi_slopes()
    w = _prep_weights(norm1_g, w_in, a_out_g, b_out_g, sink, w_out, norm2_g, w_pq, sub_keys, expert_u, expert_v,
                      normf_g)
    return (_trunk(x_prompt, w, slopes_a, slopes_b), _trunk(x_sample, w, slopes_a, slopes_b))
```

```python
import functools
import math

import jax
import jax.numpy as jnp
from jax import lax
from jax.experimental import pallas as pl
from jax.experimental.pallas import tpu as pltpu

F32 = jnp.float32
BF16 = jnp.bfloat16

HEAD_DIM = 64
N_HEADS_A = 16
N_HEADS_B = 16
N_KV_B = 4
WIDTH_A = N_HEADS_A * HEAD_DIM
WIDTH_B = N_HEADS_B * HEAD_DIM
KV_WIDTH_B = N_KV_B * HEAD_DIM
DILATED_PATTERNS = ((128, 1), (512, 4), (2048, 16))
LOCAL_WINDOW = 128
PEER_HEADS = 8
PEER_NKEYS = 128
PEER_TOPK = 16
NOT_RANKED = 127.0
RMS_EPS = 1e-6
MASKED_DIST = -1e30

VMEM_LIMIT_BYTES = 56 * 1024 * 1024
QUERY_ROWS = 128
PAIRS_PER_GROUP = 4


def _alibi_slopes():
    n = N_HEADS_A + N_HEADS_B
    s = [2.0 ** (-8.0 * i / n) for i in range(1, n + 1)]
    return tuple(s[0::2]), tuple(s[1::2])


def _rms(x, g):
    return x * lax.rsqrt(jnp.mean(x * x, axis=-1, keepdims=True) + RMS_EPS) * g


def _gelu(x):
    return 0.5 * x * (1.0 + lax.erf(x * math.sqrt(0.5)))


def _params(*sem, flags=None):
    return pltpu.CompilerParams(dimension_semantics=sem, vmem_limit_bytes=VMEM_LIMIT_BYTES, flags=flags)


def _norm_proj_kernel(x_ref, g_ref, w_ref, o_ref, xn_ref):
    @pl.when(pl.program_id(1) == 0)
    def _():
        xn_ref[...] = _rms(x_ref[...], g_ref[...]).astype(BF16)

    o_ref[...] = jnp.dot(xn_ref[...], w_ref[...], preferred_element_type=F32).astype(o_ref.dtype)


def _norm_proj(x2, g, w, *, tm, tn):
    T, D = x2.shape
    N = w.shape[1]
    return pl.pallas_call(
        _norm_proj_kernel,
        grid=(T // tm, N // tn),
        in_specs=[
            pl.BlockSpec((tm, D), lambda i, j: (i, 0)),
            pl.BlockSpec((1, D), lambda i, j: (0, 0)),
            pl.BlockSpec((D, tn), lambda i, j: (0, j)),
        ],
        out_specs=pl.BlockSpec((tm, tn), lambda i, j: (i, j)),
        out_shape=jax.ShapeDtypeStruct((T, N), BF16),
        scratch_shapes=[pltpu.VMEM((tm, D), BF16)],
        compiler_params=_params("parallel", "arbitrary"),
        name="norm_proj",
    )(x2, g, w)


def _band_attn_kernel(*refs, half, tq, seq_len, n_q_heads, group, slopes, step, with_sink, with_lse):
    q_ref, kl_ref, km_ref, kr_ref, vl_ref, vm_ref, vr_ref = refs[:7]
    pos = 7
    sink_ref = None
    if with_sink:
        sink_ref = refs[pos]
        pos += 1
    o_ref = refs[pos]
    pos += 1
    lse_ref = None
    if with_lse:
        lse_ref = refs[pos]
        pos += 1
    kwin, vwin = refs[pos], refs[pos + 1]
    pair_w = 2 * HEAD_DIM

    if group == 1:
        kwin[0:half] = kl_ref[0]
        kwin[half:half + tq] = km_ref[0]
        kwin[half + tq:] = kr_ref[0]
        vwin[0:half] = vl_ref[0]
        vwin[half:half + tq] = vm_ref[0]
        vwin[half + tq:] = vr_ref[0]
    else:
        for kv in range(n_q_heads // group):
            cols = slice(kv * HEAD_DIM, (kv + 1) * HEAD_DIM)
            for win, parts in ((kwin, (kl_ref, km_ref, kr_ref)), (vwin, (vl_ref, vm_ref, vr_ref))):
                x = jnp.concatenate([parts[0][0, :, cols], parts[1][0, :, cols], parts[2][0, :, cols]], axis=0)
                win[:, kv * pair_w:(kv + 1) * pair_w] = jnp.concatenate([x, x], axis=1)

    q0 = pl.program_id(1) * tq
    sb_rows = min(QUERY_ROWS, tq)
    kw = sb_rows + 2 * half
    n_pairs = n_q_heads // 2
    lower = lax.broadcasted_iota(jnp.int32, (sb_rows, pair_w), 1) < HEAD_DIM
    first_head = lax.broadcasted_iota(jnp.int32, (2 * sb_rows, 1), 0) < sb_rows
    ones_w = jnp.ones((kw, pair_w), BF16)
    for sb in range(tq // sb_rows):
        r0 = sb * sb_rows
        rows = lax.broadcasted_iota(jnp.int32, (sb_rows, kw), 0)
        cols = lax.broadcasted_iota(jnp.int32, (sb_rows, kw), 1)
        rel = cols - half - rows
        kpos = q0 + r0 - half + cols
        valid = (jnp.abs(rel) <= half) & (kpos >= 0) & (kpos < seq_len)
        neg_dist = jnp.where(valid, -(jnp.abs(rel) * step).astype(F32), MASKED_DIST)
        for p0 in range(0, n_pairs, PAIRS_PER_GROUP):
            pairs = range(p0, min(p0 + PAIRS_PER_GROUP, n_pairs))
            scores = []
            for p in pairs:
                q2 = q_ref[0, r0:r0 + sb_rows, p * pair_w:(p + 1) * pair_w] * 0.125
                zero = jnp.zeros_like(q2)
                q4 = jnp.concatenate([jnp.where(lower, q2, zero), jnp.where(lower, zero, q2)], axis=0)
                kcol = (2 * p // group) if group > 1 else p
                kt = kwin[r0:r0 + kw, kcol * pair_w:(kcol + 1) * pair_w]
                s = lax.dot_general(q4, kt, (((1,), (1,)), ((), ())), preferred_element_type=F32)
                bias = jnp.concatenate([slopes[2 * p] * neg_dist, slopes[2 * p + 1] * neg_dist], axis=0)
                scores.append(s + bias)
            s = jnp.stack(scores)
            m = jnp.max(s, axis=2, keepdims=True)
            if with_sink:
                sk = jnp.stack([jnp.where(first_head, sink_ref[2 * p], sink_ref[2 * p + 1]) for p in pairs])
                m = jnp.maximum(m, sk)
            pb = jnp.exp(s - m).astype(BF16)
            for i, p in enumerate(pairs):
                kcol = (2 * p // group) if group > 1 else p
                vt = vwin[r0:r0 + kw, kcol * pair_w:(kcol + 1) * pair_w]
                den = jnp.dot(pb[i], ones_w, preferred_element_type=F32)
                if with_sink:
                    den = den + jnp.exp(sk[i] - m[i])
                o2 = jnp.dot(pb[i], vt, preferred_element_type=F32) * (1.0 / den)
                o_ref[0, r0:r0 + sb_rows, p * pair_w:(p + 1) * pair_w] = jnp.where(
                    lower, o2[0:sb_rows], o2[sb_rows:]).astype(o_ref.dtype)
                if with_lse:
                    lse = m[i] + jnp.log(den)
                    lse_ref[0, r0:r0 + sb_rows, p * pair_w:(p + 1) * pair_w] = jnp.where(
                        lower, lse[0:sb_rows], lse[sb_rows:])


def _band_attn(qkv, *, q_col, k_col, v_col, n_q_heads, n_kv_heads, half, step, slopes, sink=None, with_lse):
    NS, L, _ = qkv.shape
    wq = n_q_heads * HEAD_DIM
    wk = n_kv_heads * HEAD_DIM
    tq = min(256, L)
    r = tq // half
    n_halo = L // half
    with_sink = sink is not None

    def main_map(col):
        return lambda s, i: (s, i, col)

    def left_map(col):
        return lambda s, i: (s, jnp.maximum(i * r - 1, 0), col)

    def right_map(col):
        return lambda s, i: (s, jnp.minimum((i + 1) * r, n_halo - 1), col)

    in_specs = [
        pl.BlockSpec((1, tq, wq), main_map(q_col)),
        pl.BlockSpec((1, half, wk), left_map(k_col)),
        pl.BlockSpec((1, tq, wk), main_map(k_col)),
        pl.BlockSpec((1, half, wk), right_map(k_col)),
        pl.BlockSpec((1, half, wk), left_map(v_col)),
        pl.BlockSpec((1, tq, wk), main_map(v_col)),
        pl.BlockSpec((1, half, wk), right_map(v_col)),
    ]
    args = [qkv] * 7
    if with_sink:
        in_specs.append(pl.BlockSpec(memory_space=pltpu.SMEM))
        args.append(sink)
    out_block = pl.BlockSpec((1, tq, wq), lambda s, i: (s, i, 0))
    out_specs = [out_block]
    out_shape = [jax.ShapeDtypeStruct((NS, L, wq), F32)]
    if with_lse:
        out_specs.append(out_block)
        out_shape.append(jax.ShapeDtypeStruct((NS, L, wq), F32))
    kernel = functools.partial(
        _band_attn_kernel, half=half, tq=tq, seq_len=L, n_q_heads=n_q_heads, group=n_q_heads // n_kv_heads,
        slopes=slopes, step=step, with_sink=with_sink, with_lse=with_lse)
    return pl.pallas_call(
        kernel,
        grid=(NS, L // tq),
        in_specs=in_specs,
        out_specs=out_specs,
        out_shape=out_shape,
        scratch_shapes=[pltpu.VMEM((tq + 2 * half, wk if n_q_heads == n_kv_heads else 2 * wk), BF16)] * 2,
        compiler_params=_params("parallel", "arbitrary"),
        name=f"band_attn_h{half}_s{step}",
    )(*args)


def _mix_out_kernel(o1, l1, o2, l2, o3, l3, ob, x_ref, ga, gb, g2, wa, wb, h_ref, hn_ref):
    a1, a2, a3 = l1[...], l2[...], l3[...]
    mx = jnp.maximum(jnp.maximum(a1, a2), a3)
    e1, e2, e3 = jnp.exp(a1 - mx), jnp.exp(a2 - mx), jnp.exp(a3 - mx)
    oa = (e1 * o1[...] + e2 * o2[...] + e3 * o3[...]) / (e1 + e2 + e3)
    na = _rms(oa, ga[...]).astype(BF16)
    nb = _rms(ob[...], gb[...]).astype(BF16)
    h = x_ref[...] + jnp.dot(na, wa[...], preferred_element_type=F32) + jnp.dot(nb, wb[...], preferred_element_type=F32)
    h_ref[...] = h
    hn_ref[...] = _rms(h, g2[...]).astype(BF16)


def _mix_out(o1, l1, o2, l2, o3, l3, ob, x2, ga, gb, g2, w_out, *, tm):
    T, D = x2.shape
    wa_rows = o1.shape[1]
    wb_rows = ob.shape[1]
    tok = lambda w: pl.BlockSpec((tm, w), lambda i: (i, 0))
    row = lambda w: pl.BlockSpec((1, w), lambda i: (0, 0))
    return pl.pallas_call(
        _mix_out_kernel,
        grid=(T // tm,),
        in_specs=[tok(wa_rows)] * 6 + [tok(wb_rows), tok(D), row(wa_rows), row(wb_rows), row(D),
                                      pl.BlockSpec((wa_rows, D), lambda i: (0, 0)),
                                      pl.BlockSpec((wb_rows, D), lambda i: (wa_rows // wb_rows, 0))],
        out_specs=[tok(D), tok(D)],
        out_shape=[jax.ShapeDtypeStruct((T, D), F32), jax.ShapeDtypeStruct((T, D), BF16)],
        compiler_params=_params("parallel"),
        name="mix_out",
    )(o1, l1, o2, l2, o3, l3, ob, x2, ga, gb, g2, w_out, w_out)


def _top_sorted(s, k):
    return _top_sorted_ranked(s, k, with_rank=False)[0]


def _top_sorted_ranked(s, k, with_rank=True):
    vals = []
    cur = s
    rank = jnp.full(s.shape, NOT_RANKED, F32) if with_rank else None
    for i in range(k):
        mx = jnp.max(cur, axis=0, keepdims=True)
        vals.append(mx)
        hit = cur >= mx
        if with_rank:
            rank = jnp.where(hit, float(i), rank)
        if i + 1 < k:
            cur = jnp.where(hit, -jnp.inf, cur)
    return vals, rank


def _peer_scores_kernel(hn_ref, wq_ref, sk_ref, c1_ref, ea_ref, r2_ref, es2_ref, qt_ref):
    qt_ref[...] = lax.dot_general(wq_ref[...], hn_ref[...], (((1,), (1,)), ((), ())),
                                  preferred_element_type=F32).astype(BF16)
    nk = PEER_NKEYS

    def head(h, carry):
        base = pl.multiple_of(h * 2 * nk, 2 * nk)
        s1 = jnp.dot(sk_ref[h, 0], qt_ref[pl.ds(base, nk), :], preferred_element_type=F32)
        s2 = jnp.dot(sk_ref[h, 1], qt_ref[pl.ds(base + nk, nk), :], preferred_element_type=F32)
        v1 = _top_sorted(s1, PEER_TOPK)
        v2, rank2 = _top_sorted_ranked(s2, PEER_TOPK)
        ranks = lax.broadcasted_iota(jnp.int32, (PEER_TOPK, s2.shape[1]), 0)
        v2_all = jnp.broadcast_to(v2[PEER_TOPK - 1], ranks.shape)
        for b in range(PEER_TOPK - 1):
            v2_all = jnp.where(ranks == b, v2[b], v2_all)
        v2_top = v2_all[0:8]
        cands = [v1[0] + v2_all] + [v1[a] + v2_top for a in range(1, PEER_TOPK)]
        cand = jnp.concatenate(cands, axis=0)
        top = _top_sorted(cand, PEER_TOPK)
        m = top[0]
        z = jnp.exp(top[0] - m)
        for i in range(1, PEER_TOPK):
            z = z + jnp.exp(top[i] - m)
        tau = top[PEER_TOPK - 1]
        count1 = jnp.zeros_like(s1)
        for b in range(PEER_TOPK):
            count1 = count1 + jnp.where(s1 + v2[b] >= tau, 1.0, 0.0)
        c1_ref[h] = count1
        ea_ref[h] = jnp.exp(s1 - v1[0])
        r2_ref[h] = rank2.astype(BF16)
        es2_ref[h] = (jnp.exp(s2 - v2[0]) / z).astype(BF16)
        return carry

    lax.fori_loop(0, PEER_HEADS, head, 0)


def _peer_scores(hn, wq_t, sk, *, tm):
    T, D = hn.shape
    Q = wq_t.shape[0]
    out_block = pl.BlockSpec((PEER_HEADS, PEER_NKEYS, tm), lambda i: (0, 0, i))
    out_sds = [jax.ShapeDtypeStruct((PEER_HEADS, PEER_NKEYS, T), dt) for dt in (F32, F32, BF16, BF16)]
    return pl.pallas_call(
        _peer_scores_kernel,
        grid=(T // tm,),
        in_specs=[
            pl.BlockSpec((tm, D), lambda i: (i, 0)),
            pl.BlockSpec((Q, D), lambda i: (0, 0)),
            pl.BlockSpec(sk.shape, lambda i: (0, 0, 0, 0)),
        ],
        out_specs=[out_block] * 4,
        out_shape=out_sds,
        scratch_shapes=[pltpu.VMEM((Q, tm), BF16)],
        compiler_params=_params("parallel"),
        name="peer_scores",
    )(hn, wq_t, sk)


P_ROWS = 16


def _routing_gelu_slab(act_ref, p_ref, row0, i1, c1_ref, ea_ref, r2_ref, es2_ref):
    nk = PEER_NKEYS
    zero = jnp.zeros((), BF16)
    rows = [(c1_ref[h, pl.ds(i1, 1), :].astype(BF16), ea_ref[h, pl.ds(i1, 1), :].astype(BF16))
            for h in range(PEER_HEADS)]
    for t in range(nk // P_ROWS):
        k2 = slice(t * P_ROWS, (t + 1) * P_ROWS)
        wsum = None
        for h in range(PEER_HEADS):
            c1, ea = rows[h]
            w = jnp.where(r2_ref[h, k2, :] < c1, ea * es2_ref[h, k2, :], zero)
            wsum = w if wsum is None else wsum + w
        r = slice(row0 + t * P_ROWS, row0 + (t + 1) * P_ROWS)
        p_ref[r, :] = wsum * _gelu(act_ref[r, :]).astype(BF16)


def _peer_dense_kernel(hn_ref, u_first_ref, u_next_ref, vt_prev_ref, vt_last_ref, c1_ref, ea_ref, r2_ref, es2_ref,
                       h_ref, gf_ref, y_ref, acc_ref, act0, act1, p0, p1, hnt_ref):
    e = pl.program_id(1)
    last = pl.num_programs(1) - 1
    act_bufs, p_bufs = (act0, act1), (p0, p1)
    te, tm = act0.shape
    d_model = acc_ref.shape[0]
    nk = PEER_NKEYS
    n_i1 = te // nk
    n_half = 2
    tw = tm // n_half
    n_vt = 2 * n_i1
    vr = d_model // n_vt

    @pl.when(e == 0)
    def _():
        acc_ref[...] = jnp.zeros_like(acc_ref)
        p0[...] = jnp.zeros_like(p0)
        hnt_ref[...] = hn_ref[...].T
        act0[...] = jnp.dot(u_first_ref[...], hnt_ref[...], preferred_element_type=F32)

    def step(q):
        for j in range(n_i1):
            if j % (n_i1 // n_half) == 0:
                c = j // (n_i1 // n_half)
                act_bufs[1 - q][:, c * tw:(c + 1) * tw] = jnp.dot(
                    u_next_ref[...], hnt_ref[:, c * tw:(c + 1) * tw], preferred_element_type=F32)
            for k in range(2 * j, 2 * j + 2):
                acc_ref[k * vr:(k + 1) * vr] += jnp.dot(vt_prev_ref[k * vr:(k + 1) * vr], p_bufs[q][...],
                                                        preferred_element_type=F32)
            _routing_gelu_slab(act_bufs[q], p_bufs[1 - q], j * nk, e * n_i1 + j, c1_ref, ea_ref, r2_ref, es2_ref)

    pl.when(e % 2 == 0)(functools.partial(step, 0))
    pl.when(e % 2 == 1)(functools.partial(step, 1))

    @pl.when(e == last)
    def _():
        peer_t = acc_ref[...] + jnp.dot(vt_last_ref[...], p0[...], preferred_element_type=F32)
        y_ref[...] = _rms(h_ref[...] + peer_t.T, gf_ref[...])


def _peer_dense(hn, u, vt, c1, ea, r2, es2, h, gf, *, tm, te):
    T, D = hn.shape
    E = u.shape[0]
    n_blocks = E // te
    assert n_blocks % 2 == 0
    thr = pl.BlockSpec((PEER_HEADS, PEER_NKEYS, tm), lambda i, e: (0, 0, i), pipeline_mode=pl.Buffered(1))
    return pl.pallas_call(
        _peer_dense_kernel,
        grid=(T // tm, n_blocks),
        in_specs=[
            pl.BlockSpec((tm, D), lambda i, e: (i, 0)),
            pl.BlockSpec((te, D), lambda i, e: (0, 0), pipeline_mode=pl.Buffered(1)),
            pl.BlockSpec((te, D), lambda i, e: (jnp.minimum(e + 1, n_blocks - 1), 0)),
            pl.BlockSpec((D, te), lambda i, e: (0, jnp.maximum(e - 1, 0))),
            pl.BlockSpec((D, te), lambda i, e: (0, n_blocks - 1), pipeline_mode=pl.Buffered(1)),
            thr, thr, thr, thr,
            pl.BlockSpec((tm, D), lambda i, e: (i, 0)),
            pl.BlockSpec((1, D), lambda i, e: (0, 0)),
        ],
        out_specs=pl.BlockSpec((tm, D), lambda i, e: (i, 0)),
        out_shape=jax.ShapeDtypeStruct((T, D), F32),
        scratch_shapes=[pltpu.VMEM((D, tm), F32), pltpu.VMEM((te, tm), F32), pltpu.VMEM((te, tm), F32),
                        pltpu.VMEM((te, tm), BF16), pltpu.VMEM((te, tm), BF16), pltpu.VMEM((D, tm), BF16)],
        compiler_params=_params("parallel", "arbitrary"),
        name="peer_dense",
    )(hn, u, u, vt, vt, c1, ea, r2, es2, h, gf)


def _split_residues(a, d):
    B, S, C = a.shape
    return a.reshape(B, S // d, d, C).transpose(0, 2, 1, 3).reshape(B * d, S // d, C)


def _merge_residues(a, d):
    BD, L, C = a.shape
    B = BD // d
    return a.reshape(B, d, L, C).transpose(0, 2, 1, 3).reshape(B, L * d, C)


def _trunk(x, w, slopes_a, slopes_b):
    B, S, D = x.shape
    T = B * S
    x2 = x.reshape(T, D)
    proj = _norm_proj(x2, w["norm1_g"], w["w_in"], tm=512, tn=1536).reshape(B, S, -1)

    outs = []
    for window, d in DILATED_PATTERNS:
        half = (window // 2) // d
        qkv = proj if d == 1 else _split_residues(proj[:, :, :3 * WIDTH_A], d)
        o, lse = _band_attn(qkv, q_col=0, k_col=1, v_col=2, n_q_heads=N_HEADS_A, n_kv_heads=N_HEADS_A,
                            half=half, step=d, slopes=slopes_a, with_lse=True)
        if d != 1:
            o, lse = _merge_residues(o, d), _merge_residues(lse, d)
        outs += [o.reshape(T, WIDTH_A), lse.reshape(T, WIDTH_A)]
    kb_col = (3 * WIDTH_A + WIDTH_B) // KV_WIDTH_B
    (ob,) = _band_attn(proj, q_col=3, k_col=kb_col, v_col=kb_col + 1, n_q_heads=N_HEADS_B, n_kv_heads=N_KV_B,
                       half=LOCAL_WINDOW, step=1, slopes=slopes_b, sink=w["sink"], with_lse=False)

    h, hn = _mix_out(*outs, ob.reshape(T, WIDTH_B), x2, w["a_out_g"], w["b_out_g"], w["norm2_g"], w["w_out"], tm=256)
    c1, ea, r2, es2 = _peer_scores(hn, w["w_pq_t"], w["sub_keys"], tm=256)
    y = _peer_dense(hn, w["expert_u"], w["expert_v_t"], c1, ea, r2, es2, h, w["normf_g"], tm=512, te=512)
    return y.reshape(B, S, D)


def _prep_weights(norm1_g, w_in, a_out_g, b_out_g, sink, w_out, norm2_g, w_pq, sub_keys, expert_u, expert_v, normf_g):
    assert norm1_g.shape[0] == 1, "single-layer trunk"
    return {
        "norm1_g": norm1_g[0][None].astype(F32),
        "w_in": w_in[0].astype(BF16),
        "a_out_g": a_out_g[0][None].astype(F32),
        "b_out_g": b_out_g[0][None].astype(F32),
        "sink": sink[0].astype(F32),
        "w_out": w_out[0].astype(BF16),
        "norm2_g": norm2_g[0][None].astype(F32),
        "w_pq_t": w_pq[0].T.astype(BF16),
        "sub_keys": sub_keys[0].astype(BF16),
        "expert_u": expert_u[0].astype(BF16),
        "expert_v_t": expert_v[0].T.astype(BF16),
        "normf_g": normf_g[None].astype(F32),
    }


def kernel(x_prompt, x_sample, norm1_g, w_in, a_out_g, b_out_g, sink, w_out, norm2_g, w_pq, sub_keys, expert_u,
           expert_v, normf_g):
    slopes_a, slopes_b = _alibi_slopes()
    w = _prep_weights(norm1_g, w_in, a_out_g, b_out_g, sink, w_out, norm2_g, w_pq, sub_keys, expert_u, expert_v,
                      normf_g)
    return (_trunk(x_prompt, w, slopes_a, slopes_b), _trunk(x_sample, w, slopes_a, slopes_b))
```
